```python
import math
import jax, jax.numpy as jnp
from jax import lax
import numpy as np

D_MODEL = 4096
BATCH = 4
SEQ = 2048
DEPTH = 1

CHUNK = 64
D_MIX = D_MODEL
D_SSM = D_MIX // 2
D_CONV = D_MIX - D_SSM
SSM_GROUP = 16
N_SSM_GROUPS = D_SSM // SSM_GROUP
SSM_STATE = 64
CONV_WIDTH = 31
D_FF = -(-(8 * D_MODEL) // (3 * 256)) * 256
D_IN = D_SSM + 2 * D_CONV
EPS = 1e-6
DT_MIN = 1e-3
DT_MAX = 1e-1

kernel_name = "hybrid_s5_conformer_conv_sandwich_block"


def rmsnorm(x, g):
    xf = x.astype(jnp.float32)
    y = xf * lax.rsqrt(jnp.mean(xf * xf, axis=-1, keepdims=True) + EPS)
    return (y * g.astype(jnp.float32)).astype(x.dtype)


def layernorm(x, g, b):
    xf = x.astype(jnp.float32)
    mu = jnp.mean(xf, axis=-1, keepdims=True)
    xc = xf - mu
    y = xc * lax.rsqrt(jnp.mean(xc * xc, axis=-1, keepdims=True) + EPS)
    return (y * g.astype(jnp.float32) + b.astype(jnp.float32)).astype(x.dtype)


def _scan_combine(left, right):
    a1r, a1i, b1r, b1i = left
    a2r, a2i, b2r, b2i = right
    return (a2r * a1r - a2i * a1i,
            a2r * a1i + a2i * a1r,
            a2r * b1r - a2i * b1i + b2r,
            a2r * b1i + a2i * b1r + b2i)


def s5_mixer(u, a_re, a_im, log_dt, b_re, b_im, c_re, c_im, d_skip, w_glu, b_glu):
    bsz, seq, _ = u.shape
    f32 = jnp.float32
    uf = u.astype(f32).reshape(bsz, seq, N_SSM_GROUPS, SSM_GROUP)
    lam_re, lam_im = a_re.astype(f32), a_im.astype(f32)
    dt = jnp.exp(log_dt.astype(f32))[:, None]
    z_re, z_im = lam_re * dt, lam_im * dt
    mag = jnp.exp(z_re)
    abar_re, abar_im = mag * jnp.cos(z_im), mag * jnp.sin(z_im)
    den = lam_re * lam_re + lam_im * lam_im
    num_re, num_im = abar_re - 1.0, abar_im
    coef_re = (num_re * lam_re + num_im * lam_im) / den
    coef_im = (num_im * lam_re - num_re * lam_im) / den
    bu_re = jnp.einsum('blgh,gph->blgp', uf, b_re.astype(f32))
    bu_im = jnp.einsum('blgh,gph->blgp', uf, b_im.astype(f32))
    in_re = coef_re * bu_re - coef_im * bu_im
    in_im = coef_re * bu_im + coef_im * bu_re
    ar = jnp.broadcast_to(abar_re, (1, seq) + abar_re.shape)
    ai = jnp.broadcast_to(abar_im, (1, seq) + abar_im.shape)
    _, _, s_re, s_im = lax.associative_scan(_scan_combine, (ar, ai, in_re, in_im), axis=1)
    y = (jnp.einsum('blgp,ghp->blgh', s_re, c_re.astype(f32))
         - jnp.einsum('blgp,ghp->blgh', s_im, c_im.astype(f32)))
    y = y + d_skip.astype(f32).reshape(N_SSM_GROUPS, SSM_GROUP) * uf
    y = jax.nn.gelu(y.reshape(bsz, seq, D_SSM)).astype(u.dtype)
    return y * jax.nn.sigmoid(y @ w_glu + b_glu)


def conformer_conv(val, gate, w_dw, b_dw, ln_g, ln_b):
    h = val * jax.nn.sigmoid(gate)
    h = lax.conv_general_dilated(
        h, w_dw[:, None, :].astype(h.dtype), window_strides=(1,),
        padding=[(CONV_WIDTH - 1, 0)],
        dimension_numbers=('NWC', 'WIO', 'NWC'),
        feature_group_count=D_CONV) + b_dw
    h = layernorm(h, ln_g, ln_b)
    return jax.nn.silu(h)


def setup_inputs(seed: int = 0) -> dict:
    key = jax.random.key(seed)
    ks = jax.random.split(key, 32)
    f32 = jnp.float32
    L = DEPTH

    def nrm(k, shape, scale):
        return jax.random.normal(k, shape, f32) * scale

    def gain(k, dim):
        return 1.0 + 0.01 * jax.random.normal(k, (L, dim), f32)

    x = jax.random.normal(ks[0], (BATCH, SEQ, D_MODEL), f32)
    n_idx = jnp.arange(SSM_STATE, dtype=f32)
    a_re = -0.5 + 0.01 * jax.random.normal(ks[3], (L, N_SSM_GROUPS, SSM_STATE), f32)
    a_im = math.pi * n_idx[None, None, :] + 0.01 * jax.random.normal(ks[4], (L, N_SSM_GROUPS, SSM_STATE), f32)
    log_dt = jax.random.uniform(ks[5], (L, N_SSM_GROUPS), f32, math.log(DT_MIN), math.log(DT_MAX))
    return {
        "x": x,
        "ln_pre_mix": gain(ks[1], D_MODEL),
        "w_in": nrm(ks[2], (L, D_MODEL, D_IN), D_MODEL ** -0.5),
        "ssm_a_re": a_re,
        "ssm_a_im": a_im,
        "ssm_log_dt": log_dt,
        "ssm_b_re": nrm(ks[6], (L, N_SSM_GROUPS, SSM_STATE, SSM_GROUP), (2.0 * SSM_GROUP) ** -0.5),
        "ssm_b_im": nrm(ks[7], (L, N_SSM_GROUPS, SSM_STATE, SSM_GROUP), (2.0 * SSM_GROUP) ** -0.5),
        "ssm_c_re": nrm(ks[8], (L, N_SSM_GROUPS, SSM_GROUP, SSM_STATE), (2.0 * SSM_STATE) ** -0.5),
        "ssm_c_im": nrm(ks[9], (L, N_SSM_GROUPS, SSM_GROUP, SSM_STATE), (2.0 * SSM_STATE) ** -0.5),
        "ssm_d": nrm(ks[10], (L, D_SSM), 1.0),
        "ssm_w_glu": nrm(ks[11], (L, D_SSM, D_SSM), D_SSM ** -0.5),
        "ssm_b_glu": nrm(ks[12], (L, D_SSM), 0.01),
        "conv_w_dw": nrm(ks[13], (L, CONV_WIDTH, D_CONV), CONV_WIDTH ** -0.5),
        "conv_b_dw": nrm(ks[14], (L, D_CONV), 0.01),
        "conv_ln_g": gain(ks[15], D_CONV),
        "conv_ln_b": nrm(ks[16], (L, D_CONV), 0.01),
        "norm_ssm_out": gain(ks[17], D_SSM),
        "norm_conv_out": gain(ks[18], D_CONV),
        "w_out": nrm(ks[19], (L, D_MIX, D_MODEL), D_MIX ** -0.5),
        "ln_post_mix": gain(ks[20], D_MODEL),
        "ln_pre_ffn": gain(ks[21], D_MODEL),
        "w_gate": nrm(ks[22], (L, D_MODEL, D_FF), D_MODEL ** -0.5),
        "w_up": nrm(ks[23], (L, D_MODEL, D_FF), D_MODEL ** -0.5),
        "w_down": nrm(ks[24], (L, D_FF, D_MODEL), D_FF ** -0.5),
        "ln_post_ffn": gain(ks[25], D_MODEL),
    }


def reference(x, ln_pre_mix, w_in, ssm_a_re, ssm_a_im, ssm_log_dt, ssm_b_re, ssm_b_im,
              ssm_c_re, ssm_c_im, ssm_d, ssm_w_glu, ssm_b_glu, conv_w_dw, conv_b_dw,
              conv_ln_g, conv_ln_b, norm_ssm_out, norm_conv_out, w_out, ln_post_mix,
              ln_pre_ffn, w_gate, w_up, w_down, ln_post_ffn):
    for l in range(DEPTH):
        h = rmsnorm(x, ln_pre_mix[l])
        p = h @ w_in[l]
        u_ssm = p[..., :D_SSM]
        conv_val = p[..., D_SSM:D_SSM + D_CONV]
        conv_gate = p[..., D_SSM + D_CONV:]
        y_ssm = s5_mixer(u_ssm, ssm_a_re[l], ssm_a_im[l], ssm_log_dt[l], ssm_b_re[l], ssm_b_im[l],
                         ssm_c_re[l], ssm_c_im[l], ssm_d[l], ssm_w_glu[l], ssm_b_glu[l])
        y_conv = conformer_conv(conv_val, conv_gate, conv_w_dw[l], conv_b_dw[l],
                                conv_ln_g[l], conv_ln_b[l])
        mixed = jnp.concatenate([rmsnorm(y_ssm, norm_ssm_out[l]),
                                 rmsnorm(y_conv, norm_conv_out[l])], axis=-1)
        x = x + rmsnorm(mixed @ w_out[l], ln_post_mix[l])
        h = rmsnorm(x, ln_pre_ffn[l])
        f = (jax.nn.silu(h @ w_gate[l]) * (h @ w_up[l])) @ w_down[l]
        x = x + rmsnorm(f, ln_post_ffn[l])
    return x
```

```python
import functools
import math

import jax
import jax.numpy as jnp
from jax import lax
from jax.experimental import pallas as pl
from jax.experimental.pallas import tpu as pltpu

EPS = 1e-6
SSM_GROUP = 16
SSM_STATE = 64
CONV_WIDTH = 31
CH = 8
GB = 16
BLK = GB * SSM_GROUP
NST = GB * 2 * SSM_STATE
HALF = GB * SSM_STATE
SCOL = 256
HALO = 32

_F32 = jnp.float32
_BF16 = jnp.bfloat16


def _params(vmem_mb, n_axes):
    return pltpu.CompilerParams(
        dimension_semantics=("arbitrary",) * n_axes,
        vmem_limit_bytes=vmem_mb * 1024 * 1024)


def _rms(v, g):
    return v * lax.rsqrt(jnp.mean(v * v, axis=-1, keepdims=True) + EPS) * g


def _in_proj_kernel(x_ref, g_ref, w_ref, o_ref, h_ref):
    @pl.when(pl.program_id(1) == 0)
    def _():
        h_ref[...] = _rms(x_ref[...], g_ref[...]).astype(_BF16)

    o_ref[...] = jnp.dot(h_ref[...], w_ref[...], preferred_element_type=_F32)


def _in_proj(x2, g, w_bf, bm=512, bn=512):
    m, d = x2.shape
    n = w_bf.shape[1]
    return pl.pallas_call(
        _in_proj_kernel,
        grid=(m // bm, n // bn),
        in_specs=[pl.BlockSpec((bm, d), lambda i, j: (i, 0)),
                  pl.BlockSpec((1, d), lambda i, j: (0, 0)),
                  pl.BlockSpec((d, bn), lambda i, j: (0, j))],
        out_specs=pl.BlockSpec((bm, bn), lambda i, j: (i, j)),
        out_shape=jax.ShapeDtypeStruct((m, n), _F32),
        scratch_shapes=[pltpu.VMEM((bm, d), _BF16)],
        compiler_params=_params(56, 2),
        name="in_proj",
    )(x2, g, w_bf)


def _s5_prep_kernel(are_ref, aim_ref, ldt_ref, btre_ref, btim_ref, cre_ref, cim_ref,
                    bdw_ref, bdv_ref, bdk_ref, apow_ref, *, levels):
    lam_re = are_ref[...]
    lam_im = aim_ref[...]
    dt = jnp.exp(ldt_ref[...])
    z_re = lam_re * dt
    z_im = lam_im * dt

    def apower(mm):
        mag = jnp.exp(z_re * mm)
        return mag * jnp.cos(z_im * mm), mag * jnp.sin(z_im * mm)

    abar_re, abar_im = apower(1.0)
    den = lam_re * lam_re + lam_im * lam_im
    num_re, num_im = abar_re - 1.0, abar_im
    coef_re = (num_re * lam_re + num_im * lam_im) / den
    coef_im = (num_im * lam_re - num_re * lam_im) / den
    bt_re, bt_im = btre_ref[...], btim_ref[...]
    bb_re = coef_re * bt_re - coef_im * bt_im
    bb_im = coef_re * bt_im + coef_im * bt_re
    c_re, c_im = cre_ref[...], cim_ref[...]

    row_i = lax.broadcasted_iota(jnp.int32, (BLK, HALF), 0)
    col_g = lax.broadcasted_iota(jnp.int32, (BLK, HALF), 1) // SSM_STATE
    diag = row_i // SSM_GROUP == col_g
    first_row = row_i == col_g * SSM_GROUP

    def widen(v):
        return jnp.concatenate([v.reshape(BLK, SSM_STATE)] * GB, axis=1)

    def expand(v):
        return jnp.where(diag, widen(v), 0.0)

    def lane_row(v):
        wide = widen(jnp.broadcast_to(v, bt_re.shape))
        return jnp.sum(jnp.where(first_row, wide, 0.0), axis=0, keepdims=True)

    def blockdiag(v_re, v_im):
        return jnp.concatenate([expand(v_re), expand(v_im)], axis=1)

    bdb = blockdiag(bb_re, bb_im)
    for mm in range(CH):
        ar, ai = apower(float(mm))
        bdw_ref[mm] = blockdiag(ar * bb_re - ai * bb_im, ar * bb_im + ai * bb_re).astype(_BF16)
        vt = blockdiag(c_re * ar - c_im * ai, -(c_re * ai + c_im * ar))
        bdk_ref[mm] = lax.dot_general(
            bdb, vt, (((1,), (1,)), ((), ())),
            precision=lax.Precision.HIGHEST, preferred_element_type=_F32).astype(_BF16)
        ar1, ai1 = apower(float(mm + 1))
        bdv_ref[mm] = blockdiag(c_re * ar1 - c_im * ai1, -(c_re * ai1 + c_im * ar1)).astype(_BF16)
    for lv in range(levels):
        ar, ai = apower(float(CH * (1 << lv)))
        apow_ref[lv, 0:1, :] = lane_row(ar)
        apow_ref[lv, 1:2, :] = lane_row(ai)


def _s5_prep(a_re, a_im, log_dt, b_re, b_im, c_re, c_im, levels):
    g, p = a_re.shape
    h = b_re.shape[-1]
    nblk = g // GB
    vec = lambda a: a.reshape(g, 1, p)
    bt = lambda a: jnp.swapaxes(a, 1, 2)
    spec_vec = pl.BlockSpec((GB, 1, p), lambda c: (c, 0, 0))
    spec_mat = pl.BlockSpec((GB, h, p), lambda c: (c, 0, 0))
    return pl.pallas_call(
        functools.partial(_s5_prep_kernel, levels=levels),
        grid=(nblk,),
        in_specs=[spec_vec, spec_vec, pl.BlockSpec((GB, 1, 1), lambda c: (c, 0, 0)),
                  spec_mat, spec_mat, spec_mat, spec_mat],
        out_specs=[pl.BlockSpec((None, CH, BLK, NST), lambda c: (c, 0, 0, 0)),
                   pl.BlockSpec((None, CH, BLK, NST), lambda c: (c, 0, 0, 0)),
                   pl.BlockSpec((None, CH, BLK, BLK), lambda c: (c, 0, 0, 0)),
                   pl.BlockSpec((None, levels, 2, HALF), lambda c: (c, 0, 0, 0))],
        out_shape=[jax.ShapeDtypeStruct((nblk, CH, BLK, NST), _BF16),
                   jax.ShapeDtypeStruct((nblk, CH, BLK, NST), _BF16),
                   jax.ShapeDtypeStruct((nblk, CH, BLK, BLK), _BF16),
                   jax.ShapeDtypeStruct((nblk, levels, 2, HALF), _F32)],
        compiler_params=_params(56, 1),
        name="s5_prep",
    )(vec(a_re), vec(a_im), log_dt.reshape(g, 1, 1), bt(b_re), bt(b_im), c_re, c_im)


def _s5_mix_kernel(u_ref, d_ref, bdw_ref, bdv_ref, bdk_ref, apow_ref, y_ref,
                   u3_ref, ut_ref, sa_ref, sb_ref, y3_ref, *, seq, levels):
    nch = seq // CH
    nslab = BLK // 128
    for s in range(nslab):
        u3_ref[s] = u_ref[:, s * 128:(s + 1) * 128]

    def u_tau(tau):
        return jnp.concatenate(
            [u3_ref[s, pl.ds(tau, nch, stride=CH), :] for s in range(nslab)], axis=1)

    for tau in range(CH):
        ut_ref[tau] = u_tau(tau).astype(_BF16)

    pad = nch
    zeros_pad = jnp.zeros((pad, NST), _F32)
    sa_ref[0:pad, :] = zeros_pad
    sb_ref[0:pad, :] = zeros_pad
    for n0 in range(0, NST, SCOL):
        x = jnp.dot(ut_ref[0], bdw_ref[CH - 1, :, n0:n0 + SCOL], preferred_element_type=_F32)
        for tau in range(1, CH):
            x = x + jnp.dot(ut_ref[tau], bdw_ref[CH - 1 - tau, :, n0:n0 + SCOL],
                            preferred_element_type=_F32)
        sa_ref[pad:pad + nch, n0:n0 + SCOL] = x

    src, dst = sa_ref, sb_ref
    for lv in range(levels):
        dd = 1 << lv
        ar = apow_ref[lv, 0:1, :]
        ai = apow_ref[lv, 1:2, :]
        cur_re = src[pad:pad + nch, 0:HALF]
        cur_im = src[pad:pad + nch, HALF:NST]
        prv_re = src[pad - dd:pad - dd + nch, 0:HALF]
        prv_im = src[pad - dd:pad - dd + nch, HALF:NST]
        dst[pad:pad + nch, 0:HALF] = cur_re + ar * prv_re - ai * prv_im
        dst[pad:pad + nch, HALF:NST] = cur_im + ar * prv_im + ai * prv_re
        src, dst = dst, src
    s_prev = src[pad - 1:pad - 1 + nch, :].astype(_BF16)

    dvec = d_ref[...]
    for tau in range(CH):
        y = lax.dot_general(s_prev, bdv_ref[tau], (((1,), (1,)), ((), ())),
                            preferred_element_type=_F32)
        for t2 in range(tau + 1):
            y = y + jnp.dot(ut_ref[t2], bdk_ref[tau - t2], preferred_element_type=_F32)
        y = jax.nn.gelu(y + dvec * u_tau(tau))
        for s in range(nslab):
            y3_ref[s, pl.ds(tau, nch, stride=CH), :] = y[:, s * 128:(s + 1) * 128]
    for s in range(nslab):
        y_ref[:, s * 128:(s + 1) * 128] = y3_ref[s]


def _s5_mix(p, d_skip, bdw, bdv, bdk, apow, batch, seq, d_ssm):
    nblk = d_ssm // BLK
    nch = seq // CH
    levels = apow.shape[1]
    return pl.pallas_call(
        functools.partial(_s5_mix_kernel, seq=seq, levels=levels),
        grid=(nblk, batch),
        in_specs=[pl.BlockSpec((seq, BLK), lambda c, b: (b, c)),
                  pl.BlockSpec((1, BLK), lambda c, b: (0, c)),
                  pl.BlockSpec((None, CH, BLK, NST), lambda c, b: (c, 0, 0, 0),
                               pipeline_mode=pl.Buffered(1)),
                  pl.BlockSpec((None, CH, BLK, NST), lambda c, b: (c, 0, 0, 0),
                               pipeline_mode=pl.Buffered(1)),
                  pl.BlockSpec((None, CH, BLK, BLK), lambda c, b: (c, 0, 0, 0)),
                  pl.BlockSpec((None, levels, 2, HALF), lambda c, b: (c, 0, 0, 0))],
        out_specs=pl.BlockSpec((seq, BLK), lambda c, b: (b, c)),
        out_shape=jax.ShapeDtypeStruct((batch * seq, d_ssm), _F32),
        scratch_shapes=[pltpu.VMEM((BLK // 128, seq, 128), _F32),
                        pltpu.VMEM((CH, nch, BLK), _BF16),
                        pltpu.VMEM((2 * nch, NST), _F32),
                        pltpu.VMEM((2 * nch, NST), _F32),
                        pltpu.VMEM((BLK // 128, seq, 128), _F32)],
        compiler_params=_params(56, 2),
        name="s5_mix",
    )(p, d_skip, bdw, bdv, bdk, apow)


def _s5_glu_kernel(y_ref, w_ref, b_ref, g_ref, o_ref):
    y = y_ref[...]
    z = jnp.dot(y.astype(_BF16), w_ref[...], preferred_element_type=_F32) + b_ref[...]
    o_ref[...] = _rms(y * jax.nn.sigmoid(z), g_ref[...]).astype(o_ref.dtype)


def _s5_glu(y, w_bf, b, g, bm=512):
    m, d = y.shape
    row = pl.BlockSpec((1, d), lambda i: (0, 0))
    return pl.pallas_call(
        _s5_glu_kernel,
        grid=(m // bm,),
        in_specs=[pl.BlockSpec((bm, d), lambda i: (i, 0)),
                  pl.BlockSpec((d, d), lambda i: (0, 0)), row, row],
        out_specs=pl.BlockSpec((bm, d), lambda i: (i, 0)),
        out_shape=jax.ShapeDtypeStruct((m, d), _BF16),
        compiler_params=_params(56, 1),
        name="s5_glu",
    )(y, w_bf, b, g)


def _conv_mix_kernel(val_ref, gate_ref, w_ref, b_ref, lng_ref, lnb_ref, g_ref, o_ref,
                     h_ref, c_ref, *, tl):
    d = val_ref.shape[1]

    @pl.when(pl.program_id(1) == 0)
    def _():
        h_ref[0:HALO, :] = jnp.zeros((HALO, d), _F32)

    @pl.when(pl.program_id(1) != 0)
    def _():
        h_ref[0:HALO, :] = h_ref[tl:tl + HALO, :]

    h_ref[HALO:HALO + tl, :] = val_ref[...] * jax.nn.sigmoid(gate_ref[...])
    base = HALO - (CONV_WIDTH - 1)
    for c in range(d // 128):
        cs = slice(c * 128, (c + 1) * 128)
        acc = w_ref[0:1, cs] * h_ref[base:base + tl, cs]
        for k in range(1, CONV_WIDTH):
            acc = acc + w_ref[k:k + 1, cs] * h_ref[base + k:base + k + tl, cs]
        c_ref[:, cs] = acc
    v = c_ref[...] + b_ref[...]
    mu = jnp.mean(v, axis=-1, keepdims=True)
    vc = v - mu
    v = vc * lax.rsqrt(jnp.mean(vc * vc, axis=-1, keepdims=True) + EPS) * lng_ref[...] + lnb_ref[...]
    o_ref[...] = _rms(jax.nn.silu(v), g_ref[...]).astype(o_ref.dtype)


def _conv_mix(p, w_dw, b_dw, ln_g, ln_b, g, batch, seq, d_conv, col_val, col_gate, tl=256):
    nt = seq // tl
    row = pl.BlockSpec((1, d_conv), lambda b, i: (0, 0))
    return pl.pallas_call(
        functools.partial(_conv_mix_kernel, tl=tl),
        grid=(batch, nt),
        in_specs=[pl.BlockSpec((tl, d_conv), lambda b, i: (b * nt + i, col_val)),
                  pl.BlockSpec((tl, d_conv), lambda b, i: (b * nt + i, col_gate)),
                  pl.BlockSpec((CONV_WIDTH, d_conv), lambda b, i: (0, 0)),
                  row, row, row, row],
        out_specs=pl.BlockSpec((tl, d_conv), lambda b, i: (b * nt + i, 0)),
        out_shape=jax.ShapeDtypeStruct((batch * seq, d_conv), _BF16),
        scratch_shapes=[pltpu.VMEM((HALO + tl, d_conv), _F32),
                        pltpu.VMEM((tl, d_conv), _F32)],
        compiler_params=_params(40, 2),
        name="conv_mix",
    )(p, p, w_dw, b_dw, ln_g, ln_b, g)


def _out_proj_kernel(a_ref, b_ref, wa_ref, wb_ref, o_ref):
    o_ref[...] = (jnp.dot(a_ref[...], wa_ref[...], preferred_element_type=_F32)
                  + jnp.dot(b_ref[...], wb_ref[...], preferred_element_type=_F32))


def _out_proj(ms, mc, w_bf, bm=1024, bn=1024):
    m, ka = ms.shape
    kb = mc.shape[1]
    n = w_bf.shape[1]
    assert ka == kb
    return pl.pallas_call(
        _out_proj_kernel,
        grid=(m // bm, n // bn),
        in_specs=[pl.BlockSpec((bm, ka), lambda i, j: (i, 0)),
                  pl.BlockSpec((bm, kb), lambda i, j: (i, 0)),
                  pl.BlockSpec((ka, bn), lambda i, j: (0, j)),
                  pl.BlockSpec((kb, bn), lambda i, j: (1, j))],
        out_specs=pl.BlockSpec((bm, bn), lambda i, j: (i, j)),
        out_shape=jax.ShapeDtypeStruct((m, n), _F32),
        compiler_params=_params(56, 2),
        name="out_proj",
    )(ms, mc, w_bf, w_bf)


def _post_mix_kernel(x_ref, m_ref, g1_ref, g2_ref, x1_ref, h2_ref):
    x1 = x_ref[...] + _rms(m_ref[...], g1_ref[...])
    x1_ref[...] = x1
    h2_ref[...] = _rms(x1, g2_ref[...]).astype(h2_ref.dtype)


def _post_mix(x2, mm, g1, g2, bm=256):
    m, d = x2.shape
    tile = pl.BlockSpec((bm, d), lambda i: (i, 0))
    row = pl.BlockSpec((1, d), lambda i: (0, 0))
    return pl.pallas_call(
        _post_mix_kernel,
        grid=(m // bm,),
        in_specs=[tile, tile, row, row],
        out_specs=[tile, tile],
        out_shape=[jax.ShapeDtypeStruct((m, d), _F32), jax.ShapeDtypeStruct((m, d), _BF16)],
        compiler_params=_params(56, 1),
        name="post_mix",
    )(x2, mm, g1, g2)


def _gate_up_kernel(h_ref, wg_ref, wu_ref, o_ref):
    h = h_ref[...]
    g = jnp.dot(h, wg_ref[...], preferred_element_type=_F32)
    u = jnp.dot(h, wu_ref[...], preferred_element_type=_F32)
    o_ref[...] = (jax.nn.silu(g) * u).astype(o_ref.dtype)


def _gate_up(h2, wg_bf, wu_bf, bm=1024, bn=256):
    m, d = h2.shape
    f = wg_bf.shape[1]
    wspec = pl.BlockSpec((d, bn), lambda i, j: (0, j))
    return pl.pallas_call(
        _gate_up_kernel,
        grid=(m // bm, f // bn),
        in_specs=[pl.BlockSpec((bm, d), lambda i, j: (i, 0)), wspec, wspec],
        out_specs=pl.BlockSpec((bm, bn), lambda i, j: (i, j)),
        out_shape=jax.ShapeDtypeStruct((m, f), _BF16),
        compiler_params=_params(56, 2),
        name="ffn_gate_up",
    )(h2, wg_bf, wu_bf)


def _down_kernel(a_ref, w_ref, o_ref):
    o_ref[...] = jnp.dot(a_ref[...], w_ref[...], preferred_element_type=_F32)


def _down(act, wd_bf, bm=512, bn=512):
    m, f = act.shape
    n = wd_bf.shape[1]
    return pl.pallas_call(
        _down_kernel,
        grid=(m // bm, n // bn),
        in_specs=[pl.BlockSpec((bm, f), lambda i, j: (i, 0)),
                  pl.BlockSpec((f, bn), lambda i, j: (0, j))],
        out_specs=pl.BlockSpec((bm, bn), lambda i, j: (i, j)),
        out_shape=jax.ShapeDtypeStruct((m, n), _F32),
        compiler_params=_params(56, 2),
        name="ffn_down",
    )(act, wd_bf)


def _final_kernel(x1_ref, f_ref, g_ref, o_ref):
    o_ref[...] = x1_ref[...] + _rms(f_ref[...], g_ref[...])


def _final(x1, ff, g, bm=256):
    m, d = x1.shape
    tile = pl.BlockSpec((bm, d), lambda i: (i, 0))
    return pl.pallas_call(
        _final_kernel,
        grid=(m // bm,),
        in_specs=[tile, tile, pl.BlockSpec((1, d), lambda i: (0, 0))],
        out_specs=tile,
        out_shape=jax.ShapeDtypeStruct((m, d), _F32),
        compiler_params=_params(56, 1),
        name="final_residual",
    )(x1, ff, g)


def kernel(x, ln_pre_mix, w_in, ssm_a_re, ssm_a_im, ssm_log_dt, ssm_b_re, ssm_b_im, ssm_c_re, ssm_c_im, ssm_d, ssm_w_glu, ssm_b_glu, conv_w_dw, conv_b_dw, conv_ln_g, conv_ln_b, norm_ssm_out, norm_conv_out, w_out, ln_post_mix, ln_pre_ffn, w_gate, w_up, w_down, ln_post_ffn):
    batch, seq, d_model = x.shape
    depth = w_in.shape[0]
    d_ssm = ssm_d.shape[-1]
    d_conv = conv_b_dw.shape[-1]
    assert d_ssm == d_conv and d_ssm % BLK == 0 and seq % CH == 0
    nch = seq // CH
    levels = max(1, int(math.ceil(math.log2(nch))))
    xs = x.reshape(batch * seq, d_model)
    for l in range(depth):
        row = lambda a: a[l].reshape(1, -1)
        p = _in_proj(xs, row(ln_pre_mix), w_in[l].astype(_BF16))
        bdw, bdv, bdk, apow = _s5_prep(ssm_a_re[l], ssm_a_im[l], ssm_log_dt[l], ssm_b_re[l],
                                       ssm_b_im[l], ssm_c_re[l], ssm_c_im[l], levels)
        y = _s5_mix(p, row(ssm_d), bdw, bdv, bdk, apow, batch, seq, d_ssm)
        ms = _s5_glu(y, ssm_w_glu[l].astype(_BF16), row(ssm_b_glu), row(norm_ssm_out))
        mc = _conv_mix(p, conv_w_dw[l], row(conv_b_dw), row(conv_ln_g), row(conv_ln_b),
                       row(norm_conv_out), batch, seq, d_conv,
                       col_val=d_ssm // d_conv, col_gate=d_ssm // d_conv + 1)
        mm = _out_proj(ms, mc, w_out[l].astype(_BF16))
        x1, h2 = _post_mix(xs, mm, row(ln_post_mix), row(ln_pre_ffn))
        act = _gate_up(h2, w_gate[l].astype(_BF16), w_up[l].astype(_BF16))
        ff = _down(act, w_down[l].astype(_BF16))
        xs = _final(x1, ff, row(ln_post_ffn))
    return xs.reshape(batch, seq, d_model)
```

```python
import functools
import math

import jax
import jax.numpy as jnp
from jax import lax
from jax.experimental import pallas as pl
from jax.experimental.pallas import tpu as pltpu

EPS = 1e-6
SSM_GROUP = 16
SSM_STATE = 64
CONV_WIDTH = 31
CH = 8
GB = 16
BLK = GB * SSM_GROUP
NST = GB * 2 * SSM_STATE
HALF = GB * SSM_STATE
SCOL = 256
HALO = 32
LANES = 128
CONV_ROWS = 128

_F32 = jnp.float32
_BF16 = jnp.bfloat16


def _params(vmem_mb, n_axes):
    return pltpu.CompilerParams(
        dimension_semantics=("arbitrary",) * n_axes,
        vmem_limit_bytes=vmem_mb * 1024 * 1024)


def _rms(v, g):
    return v * lax.rsqrt(jnp.mean(v * v, axis=-1, keepdims=True) + EPS) * g


def _pre_norm_kernel(x_ref, g_ref, o_ref):
    o_ref[...] = _rms(x_ref[...], g_ref[...]).astype(o_ref.dtype)


def _pre_norm(x2, g, bm=256):
    m, d = x2.shape
    tile = pl.BlockSpec((bm, d), lambda i: (i, 0))
    return pl.pallas_call(
        _pre_norm_kernel,
        grid=(m // bm,),
        in_specs=[tile, pl.BlockSpec((1, d), lambda i: (0, 0))],
        out_specs=tile,
        out_shape=jax.ShapeDtypeStruct((m, d), _BF16),
        compiler_params=_params(32, 1),
        name="pre_norm",
    )(x2, g)


def _matmul_kernel(a_ref, w_ref, o_ref):
    o_ref[...] = jnp.dot(a_ref[...], w_ref[...], preferred_element_type=_F32)


def _matmul(a, w_bf, bm, bn, name):
    m, k = a.shape
    n = w_bf.shape[1]
    return pl.pallas_call(
        _matmul_kernel,
        grid=(m // bm, n // bn),
        in_specs=[pl.BlockSpec((bm, k), lambda i, j: (i, 0)),
                  pl.BlockSpec((k, bn), lambda i, j: (0, j))],
        out_specs=pl.BlockSpec((bm, bn), lambda i, j: (i, j)),
        out_shape=jax.ShapeDtypeStruct((m, n), _F32),
        compiler_params=_params(56, 2),
        name=name,
    )(a, w_bf)


def _s5_prep_kernel(are_ref, aim_ref, ldt_ref, btre_ref, btim_ref, cre_ref, cim_ref,
                    bdw_ref, bdv_ref, bdk_ref, apow_ref, *, levels):
    lam_re = are_ref[...]
    lam_im = aim_ref[...]
    dt = jnp.exp(ldt_ref[...])
    z_re = lam_re * dt
    z_im = lam_im * dt

    def apower(mm):
        mag = jnp.exp(z_re * mm)
        return mag * jnp.cos(z_im * mm), mag * jnp.sin(z_im * mm)

    abar_re, abar_im = apower(1.0)
    den = lam_re * lam_re + lam_im * lam_im
    num_re, num_im = abar_re - 1.0, abar_im
    coef_re = (num_re * lam_re + num_im * lam_im) / den
    coef_im = (num_im * lam_re - num_re * lam_im) / den
    bt_re, bt_im = btre_ref[...], btim_ref[...]
    bb_re = coef_re * bt_re - coef_im * bt_im
    bb_im = coef_re * bt_im + coef_im * bt_re
    c_re, c_im = cre_ref[...], cim_ref[...]

    row_i = lax.broadcasted_iota(jnp.int32, (BLK, HALF), 0)
    col_g = lax.broadcasted_iota(jnp.int32, (BLK, HALF), 1) // SSM_STATE
    diag = row_i // SSM_GROUP == col_g
    first_row = row_i == col_g * SSM_GROUP

    def widen(v):
        return jnp.concatenate([v.reshape(BLK, SSM_STATE)] * GB, axis=1)

    def expand(v):
        return jnp.where(diag, widen(v), 0.0)

    def lane_row(v):
        wide = widen(jnp.broadcast_to(v, bt_re.shape))
        return jnp.sum(jnp.where(first_row, wide, 0.0), axis=0, keepdims=True)

    def blockdiag(v_re, v_im):
        return jnp.concatenate([expand(v_re), expand(v_im)], axis=1)

    bdb = blockdiag(bb_re, bb_im).astype(_BF16)
    for mm in range(CH):
        ar, ai = apower(float(mm))
        bdw_ref[mm] = blockdiag(ar * bb_re - ai * bb_im, ar * bb_im + ai * bb_re).astype(_BF16)
        vt = blockdiag(c_re * ar - c_im * ai, -(c_re * ai + c_im * ar)).astype(_BF16)
        bdk_ref[mm] = lax.dot_general(
            bdb, vt, (((1,), (1,)), ((), ())), preferred_element_type=_F32).astype(_BF16)
        ar1, ai1 = apower(float(mm + 1))
        bdv_ref[mm] = blockdiag(c_re * ar1 - c_im * ai1, -(c_re * ai1 + c_im * ar1)).astype(_BF16)
    for lv in range(levels):
        ar, ai = apower(float(CH * (1 << lv)))
        apow_ref[lv, 0:1, :] = lane_row(ar)
        apow_ref[lv, 1:2, :] = lane_row(ai)


def _s5_prep(a_re, a_im, log_dt, b_re, b_im, c_re, c_im, levels):
    g, p = a_re.shape
    h = b_re.shape[-1]
    nblk = g // GB
    vec = lambda a: a.reshape(g, 1, p)
    bt = lambda a: jnp.swapaxes(a, 1, 2)
    spec_vec = pl.BlockSpec((GB, 1, p), lambda c: (c, 0, 0))
    spec_mat = pl.BlockSpec((GB, h, p), lambda c: (c, 0, 0))
    return pl.pallas_call(
        functools.partial(_s5_prep_kernel, levels=levels),
        grid=(nblk,),
        in_specs=[spec_vec, spec_vec, pl.BlockSpec((GB, 1, 1), lambda c: (c, 0, 0)),
                  spec_mat, spec_mat, spec_mat, spec_mat],
        out_specs=[pl.BlockSpec((None, CH, BLK, NST), lambda c: (c, 0, 0, 0)),
                   pl.BlockSpec((None, CH, BLK, NST), lambda c: (c, 0, 0, 0)),
                   pl.BlockSpec((None, CH, BLK, BLK), lambda c: (c, 0, 0, 0)),
                   pl.BlockSpec((None, levels, 2, HALF), lambda c: (c, 0, 0, 0))],
        out_shape=[jax.ShapeDtypeStruct((nblk, CH, BLK, NST), _BF16),
                   jax.ShapeDtypeStruct((nblk, CH, BLK, NST), _BF16),
                   jax.ShapeDtypeStruct((nblk, CH, BLK, BLK), _BF16),
                   jax.ShapeDtypeStruct((nblk, levels, 2, HALF), _F32)],
        compiler_params=_params(56, 1),
        name="s5_prep",
    )(vec(a_re), vec(a_im), log_dt.reshape(g, 1, 1), bt(b_re), bt(b_im), c_re, c_im)


def _s5_mix_kernel(u_ref, d_ref, bdw_ref, bdv_ref, bdk_ref, apow_ref, y_ref,
                   u3_ref, ut_ref, sa_ref, sb_ref, sp_ref, y3_ref, *, seq, levels):
    nch = seq // CH
    nslab = BLK // LANES
    for s in range(nslab):
        u3_ref[s] = u_ref[:, s * LANES:(s + 1) * LANES]

    def u_tau(tau):
        return jnp.concatenate(
            [u3_ref[s, pl.ds(tau, nch, stride=CH), :] for s in range(nslab)], axis=1)

    for tau in range(CH):
        ut_ref[tau] = u_tau(tau).astype(_BF16)

    pad = nch
    hslab = HALF // LANES
    zeros_pad = jnp.zeros((2 * hslab, pad, LANES), _F32)
    sa_ref[:, 0:pad, :] = zeros_pad
    sb_ref[:, 0:pad, :] = zeros_pad
    for n0 in range(0, NST, SCOL):
        x = jnp.dot(ut_ref[0], bdw_ref[CH - 1, :, n0:n0 + SCOL], preferred_element_type=_F32)
        for tau in range(1, CH):
            x = x + jnp.dot(ut_ref[tau], bdw_ref[CH - 1 - tau, :, n0:n0 + SCOL],
                            preferred_element_type=_F32)
        for t in range(SCOL // LANES):
            sa_ref[n0 // LANES + t, pad:pad + nch, :] = x[:, t * LANES:(t + 1) * LANES]

    src, dst = sa_ref, sb_ref
    for lv in range(levels):
        dd = 1 << lv
        for s in range(hslab):
            ls = slice(s * LANES, (s + 1) * LANES)
            ar = apow_ref[lv, 0:1, ls]
            ai = apow_ref[lv, 1:2, ls]
            cur_re = src[s, pad:pad + nch, :]
            cur_im = src[hslab + s, pad:pad + nch, :]
            prv_re = src[s, pad - dd:pad - dd + nch, :]
            prv_im = src[hslab + s, pad - dd:pad - dd + nch, :]
            dst[s, pad:pad + nch, :] = cur_re + ar * prv_re - ai * prv_im
            dst[hslab + s, pad:pad + nch, :] = cur_im + ar * prv_im + ai * prv_re
        src, dst = dst, src
    for s in range(2 * hslab):
        sp_ref[:, s * LANES:(s + 1) * LANES] = src[s, pad - 1:pad - 1 + nch, :].astype(_BF16)

    dvec = d_ref[...]
    for tau in range(CH):
        y = lax.dot_general(sp_ref[...], bdv_ref[tau], (((1,), (1,)), ((), ())),
                            preferred_element_type=_F32)
        for t2 in range(tau + 1):
            y = y + jnp.dot(ut_ref[t2], bdk_ref[tau - t2], preferred_element_type=_F32)
        y = jax.nn.gelu(y + dvec * u_tau(tau))
        for s in range(nslab):
            y3_ref[s, pl.ds(tau, nch, stride=CH), :] = y[:, s * LANES:(s + 1) * LANES]
    for s in range(nslab):
        y_ref[:, s * LANES:(s + 1) * LANES] = y3_ref[s]


def _s5_mix(p, d_skip, bdw, bdv, bdk, apow, batch, seq, d_ssm):
    nblk = d_ssm // BLK
    nch = seq // CH
    levels = apow.shape[1]
    return pl.pallas_call(
        functools.partial(_s5_mix_kernel, seq=seq, levels=levels),
        grid=(nblk, batch),
        in_specs=[pl.BlockSpec((seq, BLK), lambda c, b: (b, c)),
                  pl.BlockSpec((1, BLK), lambda c, b: (0, c)),
                  pl.BlockSpec((None, CH, BLK, NST), lambda c, b: (c, 0, 0, 0),
                               pipeline_mode=pl.Buffered(1)),
                  pl.BlockSpec((None, CH, BLK, NST), lambda c, b: (c, 0, 0, 0),
                               pipeline_mode=pl.Buffered(1)),
                  pl.BlockSpec((None, CH, BLK, BLK), lambda c, b: (c, 0, 0, 0)),
                  pl.BlockSpec((None, levels, 2, HALF), lambda c, b: (c, 0, 0, 0))],
        out_specs=pl.BlockSpec((seq, BLK), lambda c, b: (b, c)),
        out_shape=jax.ShapeDtypeStruct((batch * seq, d_ssm), _F32),
        scratch_shapes=[pltpu.VMEM((BLK // LANES, seq, LANES), _F32),
                        pltpu.VMEM((CH, nch, BLK), _BF16),
                        pltpu.VMEM((NST // LANES, 2 * nch, LANES), _F32),
                        pltpu.VMEM((NST // LANES, 2 * nch, LANES), _F32),
                        pltpu.VMEM((nch, NST), _BF16),
                        pltpu.VMEM((BLK // LANES, seq, LANES), _F32)],
        compiler_params=_params(56, 2),
        name="s5_mix",
    )(p, d_skip, bdw, bdv, bdk, apow)


def _s5_glu_kernel(y_ref, w_ref, b_ref, g_ref, o_ref):
    y = y_ref[...]
    z = jnp.dot(y.astype(_BF16), w_ref[...], preferred_element_type=_F32) + b_ref[...]
    o_ref[...] = _rms(y * jax.nn.sigmoid(z), g_ref[...]).astype(o_ref.dtype)


def _s5_glu(y, w_bf, b, g, bm=512):
    m, d = y.shape
    row = pl.BlockSpec((1, d), lambda i: (0, 0))
    return pl.pallas_call(
        _s5_glu_kernel,
        grid=(m // bm,),
        in_specs=[pl.BlockSpec((bm, d), lambda i: (i, 0)),
                  pl.BlockSpec((d, d), lambda i: (0, 0)), row, row],
        out_specs=pl.BlockSpec((bm, d), lambda i: (i, 0)),
        out_shape=jax.ShapeDtypeStruct((m, d), _BF16),
        compiler_params=_params(56, 1),
        name="s5_glu",
    )(y, w_bf, b, g)


def _conv_mix_kernel(val_ref, gate_ref, w_ref, b_ref, lng_ref, lnb_ref, g_ref, o_ref,
                     h_ref, c_ref, *, tl):
    d = val_ref.shape[1]
    nslab = d // LANES

    @pl.when(pl.program_id(1) == 0)
    def _():
        h_ref[:, 0:HALO, :] = jnp.zeros((nslab, HALO, LANES), _F32)

    @pl.when(pl.program_id(1) != 0)
    def _():
        h_ref[:, 0:HALO, :] = h_ref[:, tl:tl + HALO, :]

    base = HALO - (CONV_WIDTH - 1)
    for c in range(nslab):
        cs = slice(c * LANES, (c + 1) * LANES)
        h_ref[c, HALO:HALO + tl, :] = val_ref[:, cs] * jax.nn.sigmoid(gate_ref[:, cs])
        for r0 in range(0, tl, CONV_ROWS):
            acc = w_ref[0:1, cs] * h_ref[c, r0 + base:r0 + base + CONV_ROWS, :]
            for k in range(1, CONV_WIDTH):
                acc = acc + w_ref[k:k + 1, cs] * h_ref[c, r0 + base + k:r0 + base + k + CONV_ROWS, :]
            c_ref[r0:r0 + CONV_ROWS, cs] = acc
    v = c_ref[...] + b_ref[...]
    mu = jnp.mean(v, axis=-1, keepdims=True)
    vc = v - mu
    v = vc * lax.rsqrt(jnp.mean(vc * vc, axis=-1, keepdims=True) + EPS) * lng_ref[...] + lnb_ref[...]
    o_ref[...] = _rms(jax.nn.silu(v), g_ref[...]).astype(o_ref.dtype)


def _conv_mix(p, w_dw, b_dw, ln_g, ln_b, g, batch, seq, d_conv, col_val, col_gate, tl=256):
    nt = seq // tl
    row = pl.BlockSpec((1, d_conv), lambda b, i: (0, 0))
    return pl.pallas_call(
        functools.partial(_conv_mix_kernel, tl=tl),
        grid=(batch, nt),
        in_specs=[pl.BlockSpec((tl, d_conv), lambda b, i: (b * nt + i, col_val)),
                  pl.BlockSpec((tl, d_conv), lambda b, i: (b * nt + i, col_gate)),
                  pl.BlockSpec((CONV_WIDTH, d_conv), lambda b, i: (0, 0)),
                  row, row, row, row],
        out_specs=pl.BlockSpec((tl, d_conv), lambda b, i: (b * nt + i, 0)),
        out_shape=jax.ShapeDtypeStruct((batch * seq, d_conv), _BF16),
        scratch_shapes=[pltpu.VMEM((d_conv // LANES, HALO + tl, LANES), _F32),
                        pltpu.VMEM((tl, d_conv), _F32)],
        compiler_params=_params(40, 2),
        name="conv_mix",
    )(p, p, w_dw, b_dw, ln_g, ln_b, g)


def _out_proj_kernel(a_ref, b_ref, wa_ref, wb_ref, o_ref):
    o_ref[...] = (jnp.dot(a_ref[...], wa_ref[...], preferred_element_type=_F32)
                  + jnp.dot(b_ref[...], wb_ref[...], preferred_element_type=_F32))


def _out_proj(ms, mc, w_bf, bm=1024, bn=1024):
    m, ka = ms.shape
    kb = mc.shape[1]
    n = w_bf.shape[1]
    assert ka == kb
    return pl.pallas_call(
        _out_proj_kernel,
        grid=(m // bm, n // bn),
        in_specs=[pl.BlockSpec((bm, ka), lambda i, j: (i, 0)),
                  pl.BlockSpec((bm, kb), lambda i, j: (i, 0)),
                  pl.BlockSpec((ka, bn), lambda i, j: (0, j)),
                  pl.BlockSpec((kb, bn), lambda i, j: (1, j))],
        out_specs=pl.BlockSpec((bm, bn), lambda i, j: (i, j)),
        out_shape=jax.ShapeDtypeStruct((m, n), _F32),
        compiler_params=_params(56, 2),
        name="out_proj",
    )(ms, mc, w_bf, w_bf)


def _post_mix_kernel(x_ref, m_ref, g1_ref, g2_ref, x1_ref, h2_ref):
    x1 = x_ref[...] + _rms(m_ref[...], g1_ref[...])
    x1_ref[...] = x1
    h2_ref[...] = _rms(x1, g2_ref[...]).astype(h2_ref.dtype)


def _post_mix(x2, mm, g1, g2, bm=256):
    m, d = x2.shape
    tile = pl.BlockSpec((bm, d), lambda i: (i, 0))
    row = pl.BlockSpec((1, d), lambda i: (0, 0))
    return pl.pallas_call(
        _post_mix_kernel,
        grid=(m // bm,),
        in_specs=[tile, tile, row, row],
        out_specs=[tile, tile],
        out_shape=[jax.ShapeDtypeStruct((m, d), _F32), jax.ShapeDtypeStruct((m, d), _BF16)],
        compiler_params=_params(56, 1),
        name="post_mix",
    )(x2, mm, g1, g2)


def _gate_up_kernel(h_ref, wg_ref, wu_ref, o_ref):
    h = h_ref[...]
    g = jnp.dot(h, wg_ref[...].astype(_BF16), preferred_element_type=_F32)
    u = jnp.dot(h, wu_ref[...].astype(_BF16), preferred_element_type=_F32)
    o_ref[...] = (jax.nn.silu(g) * u).astype(o_ref.dtype)


def _gate_up(h2, wg, wu, bm=1024, bn=256):
    m, d = h2.shape
    f = wg.shape[1]
    wspec = pl.BlockSpec((d, bn), lambda i, j: (0, j))
    return pl.pallas_call(
        _gate_up_kernel,
        grid=(m // bm, f // bn),
        in_specs=[pl.BlockSpec((bm, d), lambda i, j: (i, 0)), wspec, wspec],
        out_specs=pl.BlockSpec((bm, bn), lambda i, j: (i, j)),
        out_shape=jax.ShapeDtypeStruct((m, f), _BF16),
        compiler_params=_params(56, 2),
        name="ffn_gate_up",
    )(h2, wg, wu)


def _final_kernel(x1_ref, f_ref, g_ref, o_ref):
    o_ref[...] = x1_ref[...] + _rms(f_ref[...], g_ref[...])


def _final(x1, ff, g, bm=256):
    m, d = x1.shape
    tile = pl.BlockSpec((bm, d), lambda i: (i, 0))
    return pl.pallas_call(
        _final_kernel,
        grid=(m // bm,),
        in_specs=[tile, tile, pl.BlockSpec((1, d), lambda i: (0, 0))],
        out_specs=tile,
        out_shape=jax.ShapeDtypeStruct((m, d), _F32),
        compiler_params=_params(56, 1),
        name="final_residual",
    )(x1, ff, g)


def kernel(x, ln_pre_mix, w_in, ssm_a_re, ssm_a_im, ssm_log_dt, ssm_b_re, ssm_b_im, ssm_c_re, ssm_c_im, ssm_d, ssm_w_glu, ssm_b_glu, conv_w_dw, conv_b_dw, conv_ln_g, conv_ln_b, norm_ssm_out, norm_conv_out, w_out, ln_post_mix, ln_pre_ffn, w_gate, w_up, w_down, ln_post_ffn):
    batch, seq, d_model = x.shape
    depth = w_in.shape[0]
    d_ssm = ssm_d.shape[-1]
    d_conv = conv_b_dw.shape[-1]
    assert d_ssm == d_conv and d_ssm % BLK == 0 and seq % CH == 0
    nch = seq // CH
    levels = max(1, int(math.ceil(math.log2(nch))))
    xs = x.reshape(batch * seq, d_model)
    for l in range(depth):
        row = lambda a: a[l].reshape(1, -1)
        h1 = _pre_norm(xs, row(ln_pre_mix))
        p = _matmul(h1, w_in[l].astype(_BF16), 1024, 1024, "in_proj")
        bdw, bdv, bdk, apow = _s5_prep(ssm_a_re[l], ssm_a_im[l], ssm_log_dt[l], ssm_b_re[l],
                                       ssm_b_im[l], ssm_c_re[l], ssm_c_im[l], levels)
        y = _s5_mix(p, row(ssm_d), bdw, bdv, bdk, apow, batch, seq, d_ssm)
        ms = _s5_glu(y, ssm_w_glu[l].astype(_BF16), row(ssm_b_glu), row(norm_ssm_out))
        mc = _conv_mix(p, conv_w_dw[l], row(conv_b_dw), row(conv_ln_g), row(conv_ln_b),
                       row(norm_conv_out), batch, seq, d_conv,
                       col_val=d_ssm // d_conv, col_gate=d_ssm // d_conv + 1)
        mm = _out_proj(ms, mc, w_out[l].astype(_BF16))
        x1, h2 = _post_mix(xs, mm, row(ln_post_mix), row(ln_pre_ffn))
        act = _gate_up(h2, w_gate[l], w_up[l])
        ff = _matmul(act, w_down[l].astype(_BF16), 512, 512, "ffn_down")
        xs = _final(x1, ff, row(ln_post_ffn))
    return xs.reshape(batch, seq, d_model)
```
